```python
import math
import jax, jax.numpy as jnp
from jax import lax
import numpy as np

D_MODEL = 1024
BATCH = 8
SEQ = 4096
DEPTH = 4

N_MIXERS = 3
N_MLA_LAYERS = (DEPTH + 2) // 3
N_RET_LAYERS = (DEPTH + 1) // 3
N_GLA_LAYERS = DEPTH // 3

DEEPNORM_ALPHA = (2.0 * DEPTH) ** 0.25
DEEPNORM_BETA = (8.0 * DEPTH) ** -0.25
LN_EPS = 1e-5
RMS_EPS = 1e-6

FFN_HIDDEN = -(-8 * D_MODEL // (3 * 256)) * 256

ROPE_BASE = 10000.0

MLA_HEADS = 8
MLA_NOPE = 128
MLA_ROPE = 64
MLA_V = 128
MLA_Q_RANK = 512
MLA_KV_RANK = 256
MLA_Q_BLOCK = 128

RET_HEADS = 4
RET_QK_DIM = D_MODEL // RET_HEADS
RET_V_DIM = 2 * RET_QK_DIM
RET_CHUNK = 128

GLA_HEADS = 4
GLA_KEY_DIM = (D_MODEL // 2) // GLA_HEADS
GLA_V_DIM = D_MODEL // GLA_HEADS
GLA_GATE_RANK = 16
GLA_TAU = 16.0
GLA_CHUNK = 64

kernel_name = "hybrid_mla_retention_gla_deepnorm_encoder"


def layer_norm(x, g, b):
    xf = x.astype(jnp.float32)
    mu = jnp.mean(xf, axis=-1, keepdims=True)
    var = jnp.mean(jnp.square(xf - mu), axis=-1, keepdims=True)
    return ((xf - mu) * lax.rsqrt(var + LN_EPS) * g.astype(jnp.float32) + b.astype(jnp.float32)).astype(x.dtype)


def rms_norm(x, g):
    xf = x.astype(jnp.float32)
    ms = jnp.mean(jnp.square(xf), axis=-1, keepdims=True)
    return (xf * lax.rsqrt(ms + RMS_EPS) * g.astype(jnp.float32)).astype(x.dtype)


def apply_rotary(x, positions):
    d = x.shape[-1]
    inv_freq = ROPE_BASE ** (-jnp.arange(0, d, 2, dtype=jnp.float32) / d)
    ang = positions.astype(jnp.float32)[..., None] * inv_freq
    cos = jnp.cos(ang)[:, :, None, :]
    sin = jnp.sin(ang)[:, :, None, :]
    xf = x.astype(jnp.float32)
    x1, x2 = xf[..., : d // 2], xf[..., d // 2:]
    return jnp.concatenate([x1 * cos - x2 * sin, x1 * sin + x2 * cos], axis=-1).astype(x.dtype)


def flip_seq(a):
    return jnp.flip(a, axis=1)


def mla_mixer(x, positions, w_down, q_norm, w_uq, kv_norm, w_ukv, w_o):
    B, S, _ = x.shape
    H = MLA_HEADS
    down = x @ w_down
    c_q = down[..., :MLA_Q_RANK]
    c_kv = down[..., MLA_Q_RANK:MLA_Q_RANK + MLA_KV_RANK]
    k_rope = down[..., MLA_Q_RANK + MLA_KV_RANK:]
    q = (rms_norm(c_q, q_norm) @ w_uq).reshape(B, S, H, MLA_NOPE + MLA_ROPE)
    q_nope = q[..., :MLA_NOPE]
    q_rope = apply_rotary(q[..., MLA_NOPE:], positions)
    kv = (rms_norm(c_kv, kv_norm) @ w_ukv).reshape(B, S, H, MLA_NOPE + MLA_V)
    k_nope = kv[..., :MLA_NOPE]
    v = kv[..., MLA_NOPE:]
    k_rope = apply_rotary(k_rope[:, :, None, :], positions)[:, :, 0, :]
    scale = (MLA_NOPE + MLA_ROPE) ** -0.5
    nb = S // MLA_Q_BLOCK
    qn_blocks = jnp.moveaxis(q_nope.reshape(B, nb, MLA_Q_BLOCK, H, MLA_NOPE), 1, 0)
    qr_blocks = jnp.moveaxis(q_rope.reshape(B, nb, MLA_Q_BLOCK, H, MLA_ROPE), 1, 0)

    def attend(blk):
        qn_b, qr_b = blk
        s = (jnp.einsum('bqhd,bkhd->bhqk', qn_b, k_nope)
             + jnp.einsum('bqhr,bkr->bhqk', qr_b, k_rope)).astype(jnp.float32) * scale
        p = jax.nn.softmax(s, axis=-1)
        return jnp.einsum('bhqk,bkhd->bqhd', p.astype(v.dtype), v)

    o = lax.map(attend, (qn_blocks, qr_blocks))
    o = jnp.moveaxis(o, 0, 1).reshape(B, S, H * MLA_V)
    return o @ w_o


def retention_chunked(q, k, v, log_gamma, strict):
    B, S, H, DK = q.shape
    DV = v.shape[-1]
    C = RET_CHUNK
    N = S // C
    qc = q.reshape(B, N, C, H, DK)
    kc = k.reshape(B, N, C, H, DK)
    vc = v.reshape(B, N, C, H, DV)
    lg = log_gamma.astype(jnp.float32)
    idx = jnp.arange(C, dtype=jnp.float32)
    diff = idx[:, None] - idx[None, :]
    mask = (diff > 0) if strict else (diff >= 0)
    dmat = jnp.where(mask[None], jnp.exp(jnp.maximum(diff, 0.0)[None] * lg[:, None, None]), 0.0)
    scores = jnp.einsum('bnihd,bnjhd->bnhij', qc, kc) * dmat[None, None]
    intra = jnp.einsum('bnhij,bnjhe->bnihe', scores, vc)
    q_decay = jnp.exp((idx[:, None] + 1.0) * lg[None])
    k_decay = jnp.exp((C - 1.0 - idx)[:, None] * lg[None])
    chunk_decay = jnp.exp(C * lg)

    def step(state, xs):
        q_n, k_n, v_n = xs
        cross = jnp.einsum('bihd,bhde->bihe', q_n, state) * q_decay[None, :, :, None]
        state = (state * chunk_decay[None, :, None, None]
                 + jnp.einsum('bjhd,bjhe->bhde', k_n * k_decay[None, :, :, None], v_n))
        return state, cross

    state0 = jnp.zeros((B, H, DK, DV), jnp.float32)
    _, cross = lax.scan(step, state0, (jnp.moveaxis(qc, 1, 0), jnp.moveaxis(kc, 1, 0), jnp.moveaxis(vc, 1, 0)))
    cross = jnp.moveaxis(cross, 0, 1)
    return (intra + cross).reshape(B, S, H, DV)


def retention_mixer(x, positions, w_in, decay_logit, gn_w, gn_b, w_o):
    B, S, _ = x.shape
    H, DK, DV = RET_HEADS, RET_QK_DIM, RET_V_DIM
    proj = x @ w_in
    q = proj[..., : H * DK].reshape(B, S, H, DK)
    k = proj[..., H * DK: 2 * H * DK].reshape(B, S, H, DK)
    v = proj[..., 2 * H * DK: 2 * H * DK + H * DV].reshape(B, S, H, DV).astype(jnp.float32)
    g = proj[..., 2 * H * DK + H * DV:]
    q = apply_rotary(q, positions).astype(jnp.float32)
    k = apply_rotary(k, positions).astype(jnp.float32) * (DK ** -0.5)
    log_gamma = jax.nn.log_sigmoid(decay_logit.astype(jnp.float32))
    y = (retention_chunked(q, k, v, log_gamma[0], False)
         + flip_seq(retention_chunked(flip_seq(q), flip_seq(k), flip_seq(v), log_gamma[1], True)))
    mu = jnp.mean(y, axis=-1, keepdims=True)
    var = jnp.mean(jnp.square(y - mu), axis=-1, keepdims=True)
    y = (y - mu) * lax.rsqrt(var + LN_EPS)
    y = y.reshape(B, S, H * DV) * gn_w.astype(jnp.float32) + gn_b.astype(jnp.float32)
    return (jax.nn.silu(g) * y.astype(x.dtype)) @ w_o


def gla_chunked(q, k, v, log_alpha, strict):
    B, S, H, DK = q.shape
    DV = v.shape[-1]
    C = GLA_CHUNK
    N = S // C
    qc = q.reshape(B, N, C, H, DK)
    kc = k.reshape(B, N, C, H, DK)
    vc = v.reshape(B, N, C, H, DV)
    b = jnp.cumsum(log_alpha.reshape(B, N, C, H, DK), axis=2)
    q_in = qc * jnp.exp(b)
    k_in = kc * jnp.exp(-b)
    idx = jnp.arange(C)
    mask = (idx[:, None] > idx[None, :]) if strict else (idx[:, None] >= idx[None, :])
    scores = jnp.where(mask, jnp.einsum('bnihd,bnjhd->bnhij', q_in, k_in), 0.0)
    intra = jnp.einsum('bnhij,bnjhe->bnihe', scores, vc)
    b_last = b[:, :, -1]
    k_state = kc * jnp.exp(b_last[:, :, None] - b)

    def step(state, xs):
        q_n, k_n, v_n, bl_n = xs
        cross = jnp.einsum('bihd,bhde->bihe', q_n, state)
        state = state * jnp.exp(bl_n)[..., None] + jnp.einsum('bjhd,bjhe->bhde', k_n, v_n)
        return state, cross

    state0 = jnp.zeros((B, H, DK, DV), jnp.float32)
    _, cross = lax.scan(step, state0, (jnp.moveaxis(q_in, 1, 0), jnp.moveaxis(k_state, 1, 0),
                                       jnp.moveaxis(vc, 1, 0), jnp.moveaxis(b_last, 1, 0)))
    cross = jnp.moveaxis(cross, 0, 1)
    return (intra + cross).reshape(B, S, H, DV)


def gla_mixer(x, w_in, w_a1, w_a2, b_a, norm_g, w_o):
    B, S, _ = x.shape
    H, DK, DV = GLA_HEADS, GLA_KEY_DIM, GLA_V_DIM
    hk, hv = H * DK, H * DV
    proj = x @ w_in
    q = proj[..., :hk].reshape(B, S, H, DK).astype(jnp.float32) * (DK ** -0.5)
    k = proj[..., hk:2 * hk].reshape(B, S, H, DK).astype(jnp.float32)
    v = proj[..., 2 * hk:2 * hk + hv].reshape(B, S, H, DV).astype(jnp.float32)
    r = proj[..., 2 * hk + hv:]

    def log_gate(z):
        logits = ((x @ w_a1[z]) @ w_a2[z] + b_a[z]).astype(jnp.float32)
        return (jax.nn.log_sigmoid(logits) / GLA_TAU).reshape(B, S, H, DK)

    y = (gla_chunked(q, k, v, log_gate(0), False)
         + flip_seq(gla_chunked(flip_seq(q), flip_seq(k), flip_seq(v), flip_seq(log_gate(1)), True)))
    y = rms_norm(y, norm_g).reshape(B, S, hv)
    return (jax.nn.silu(r) * y.astype(x.dtype)) @ w_o


def swiglu_ffn(x, w_in, w_out):
    gate, up = jnp.split(x @ w_in, 2, axis=-1)
    return (jax.nn.silu(gate) * up) @ w_out


def setup_inputs(seed: int = 0) -> dict:
    key = jax.random.key(seed)
    keys = iter(jax.random.split(key, 40))
    f32 = jnp.float32

    def w(shape, fan_in, gain=1.0):
        return jax.random.normal(next(keys), shape, f32) * (gain * fan_in ** -0.5)

    def gain_(shape):
        return 1.0 + 0.02 * jax.random.normal(next(keys), shape, f32)

    def bias_(shape):
        return 0.02 * jax.random.normal(next(keys), shape, f32)

    D, F = D_MODEL, FFN_HIDDEN
    nA, nB, nC = N_MLA_LAYERS, N_RET_LAYERS, N_GLA_LAYERS
    beta = DEEPNORM_BETA

    x = jax.random.normal(next(keys), (BATCH, SEQ, D), f32)
    offsets = jax.random.randint(next(keys), (BATCH, 1), 0, SEQ, dtype=jnp.int32)
    positions = (jnp.arange(SEQ, dtype=jnp.int32)[None, :] + offsets).astype(jnp.int32)

    ln_g = gain_((DEPTH, 2, D))
    ln_b = bias_((DEPTH, 2, D))
    ffn_w_in = w((DEPTH, D, 2 * F), D, beta)
    ffn_w_out = w((DEPTH, F, D), F, beta)

    H = MLA_HEADS
    mla_w_down = w((nA, D, MLA_Q_RANK + MLA_KV_RANK + MLA_ROPE), D)
    mla_q_norm = gain_((nA, MLA_Q_RANK))
    mla_w_uq = w((nA, MLA_Q_RANK, H * (MLA_NOPE + MLA_ROPE)), MLA_Q_RANK)
    mla_kv_norm = gain_((nA, MLA_KV_RANK))
    mla_w_ukv = jnp.concatenate([w((nA, MLA_KV_RANK, H, MLA_NOPE), MLA_KV_RANK),
                                 w((nA, MLA_KV_RANK, H, MLA_V), MLA_KV_RANK, beta)],
                                axis=-1).reshape(nA, MLA_KV_RANK, H * (MLA_NOPE + MLA_V))
    mla_w_o = w((nA, H * MLA_V, D), H * MLA_V, beta)

    rq = RET_HEADS * RET_QK_DIM
    rv = RET_HEADS * RET_V_DIM
    ret_w_in = jnp.concatenate([w((nB, D, 2 * rq), D), w((nB, D, rv), D, beta), w((nB, D, rv), D)], axis=-1)
    base_logit = jnp.asarray(np.log(2.0 ** (5 + np.arange(RET_HEADS)) - 1.0), dtype=f32)
    ret_decay_logit = base_logit[None, None, :] + 0.01 * jax.random.normal(next(keys), (nB, 2, RET_HEADS), f32)
    ret_gn_w = gain_((nB, rv))
    ret_gn_b = bias_((nB, rv))
    ret_w_o = w((nB, rv, D), rv, beta)

    gk = GLA_HEADS * GLA_KEY_DIM
    gv = GLA_HEADS * GLA_V_DIM
    gla_w_in = jnp.concatenate([w((nC, D, 2 * gk), D), w((nC, D, gv), D, beta), w((nC, D, gv), D)], axis=-1)
    gla_w_a1 = w((nC, 2, D, GLA_GATE_RANK), D)
    gla_w_a2 = w((nC, 2, GLA_GATE_RANK, gk), GLA_GATE_RANK)
    gla_b_a = bias_((nC, 2, gk))
    gla_norm = gain_((nC, GLA_V_DIM))
    gla_w_o = w((nC, gv, D), gv, beta)

    return {"x": x, "positions": positions, "ln_g": ln_g, "ln_b": ln_b,
            "ffn_w_in": ffn_w_in, "ffn_w_out": ffn_w_out,
            "mla_w_down": mla_w_down, "mla_q_norm": mla_q_norm, "mla_w_uq": mla_w_uq,
            "mla_kv_norm": mla_kv_norm, "mla_w_ukv": mla_w_ukv, "mla_w_o": mla_w_o,
            "ret_w_in": ret_w_in, "ret_decay_logit": ret_decay_logit, "ret_gn_w": ret_gn_w,
            "ret_gn_b": ret_gn_b, "ret_w_o": ret_w_o,
            "gla_w_in": gla_w_in, "gla_w_a1": gla_w_a1, "gla_w_a2": gla_w_a2, "gla_b_a": gla_b_a,
            "gla_norm": gla_norm, "gla_w_o": gla_w_o}


def reference(x, positions, ln_g, ln_b, ffn_w_in, ffn_w_out,
              mla_w_down, mla_q_norm, mla_w_uq, mla_kv_norm, mla_w_ukv, mla_w_o,
              ret_w_in, ret_decay_logit, ret_gn_w, ret_gn_b, ret_w_o,
              gla_w_in, gla_w_a1, gla_w_a2, gla_b_a, gla_norm, gla_w_o):
    for i in range(DEPTH):
        kind = i % N_MIXERS
        j = i // N_MIXERS
        if kind == 0:
            y = mla_mixer(x, positions, mla_w_down[j], mla_q_norm[j], mla_w_uq[j],
                          mla_kv_norm[j], mla_w_ukv[j], mla_w_o[j])
        elif kind == 1:
            y = retention_mixer(x, positions, ret_w_in[j], ret_decay_logit[j],
                                ret_gn_w[j], ret_gn_b[j], ret_w_o[j])
        else:
            y = gla_mixer(x, gla_w_in[j], gla_w_a1[j], gla_w_a2[j], gla_b_a[j],
                          gla_norm[j], gla_w_o[j])
        x = layer_norm(DEEPNORM_ALPHA * x + y, ln_g[i, 0], ln_b[i, 0])
        x = layer_norm(DEEPNORM_ALPHA * x + swiglu_ffn(x, ffn_w_in[i], ffn_w_out[i]), ln_g[i, 1], ln_b[i, 1])
    return x
```

```python
import functools

import jax
import jax.numpy as jnp
from jax import lax
from jax.experimental import pallas as pl
from jax.experimental.pallas import tpu as pltpu

F32 = jnp.float32
BF16 = jnp.bfloat16

DEPTH = 4
N_MIXERS = 3
DEEPNORM_ALPHA = (2.0 * DEPTH) ** 0.25
LN_EPS = 1e-5
RMS_EPS = 1e-6
ROPE_BASE = 10000.0

MLA_HEADS = 8
MLA_NOPE = 128
MLA_ROPE = 64
MLA_V = 128
MLA_Q_RANK = 512
MLA_KV_RANK = 256
MLA_QK_PAD = 256

RET_HEADS = 4
RET_QK_DIM = 256
RET_V_DIM = 512
RET_CHUNK = 128

GLA_HEADS = 4
GLA_KEY_DIM = 128
GLA_V_DIM = 256
GLA_GATE_RANK = 16
GLA_TAU = 16.0
GLA_CHUNK = 64

LANE = 128
VMEM_LIMIT = 56 * 1024 * 1024

_NT = (((1,), (1,)), ((), ()))
_TN = (((0,), (0,)), ((), ()))


def _cp(*sem):
    return pltpu.CompilerParams(dimension_semantics=sem, vmem_limit_bytes=VMEM_LIMIT)


def _log_sigmoid(x):
    return jnp.minimum(x, 0.0) - jnp.log(1.0 + jnp.exp(-jnp.abs(x)))


def _silu(x):
    return x * jax.nn.sigmoid(x)


def _rope_tab_kernel(pos_ref, freq_ref, sign_ref, cos_ref, sin_ref):
    ang = pos_ref[...] * freq_ref[...]
    cos_ref[...] = jnp.cos(ang)
    sin_ref[...] = jnp.sin(ang) * sign_ref[...]


def _rope_tables(pos, freq, sign, tm=1024):
    T = pos.shape[0]
    row = pl.BlockSpec((1, LANE), lambda i: (0, 0))
    tab = pl.BlockSpec((tm, LANE), lambda i: (i, 0))
    return pl.pallas_call(
        _rope_tab_kernel, grid=(T // tm,),
        in_specs=[pl.BlockSpec((tm, 1), lambda i: (i, 0)), row, row],
        out_specs=[tab, tab],
        out_shape=[jax.ShapeDtypeStruct((T, LANE), F32)] * 2,
        compiler_params=_cp("parallel"), name="rope_tables")(pos, freq, sign)


def _mm_kernel(a_ref, w_ref, o_ref):
    o_ref[...] = jnp.dot(a_ref[...], w_ref[...], preferred_element_type=F32).astype(o_ref.dtype)


def _mm(a, w, *, col_off, n, out_dtype, tm=1024, tn=512, name="mm"):
    M, K = a.shape
    return pl.pallas_call(
        _mm_kernel, grid=(M // tm, n // tn),
        in_specs=[pl.BlockSpec((tm, K), lambda i, j: (i, 0)),
                  pl.BlockSpec((K, tn), lambda i, j: (0, j + col_off // tn))],
        out_specs=pl.BlockSpec((tm, tn), lambda i, j: (i, j)),
        out_shape=jax.ShapeDtypeStruct((M, n), out_dtype),
        compiler_params=_cp("parallel", "parallel"), name=name)(a, w)


def _mm_ln_kernel(a_ref, w_ref, x_ref, g_ref, b_ref, of_ref, ob_ref):
    y = jnp.dot(a_ref[...], w_ref[...], preferred_element_type=F32)
    z = DEEPNORM_ALPHA * x_ref[...] + y
    mu = jnp.mean(z, axis=-1, keepdims=True)
    zc = z - mu
    var = jnp.mean(zc * zc, axis=-1, keepdims=True)
    o = zc * lax.rsqrt(var + LN_EPS) * g_ref[...] + b_ref[...]
    of_ref[...] = o
    ob_ref[...] = o.astype(BF16)


def _mm_ln(a, w, x, g, b, tm=512, name="mm_ln"):
    M, K = a.shape
    D = w.shape[1]
    row = pl.BlockSpec((1, D), lambda i: (0, 0))
    tile = pl.BlockSpec((tm, D), lambda i: (i, 0))
    return pl.pallas_call(
        _mm_ln_kernel, grid=(M // tm,),
        in_specs=[pl.BlockSpec((tm, K), lambda i: (i, 0)),
                  pl.BlockSpec((K, D), lambda i: (0, 0)), tile, row, row],
        out_specs=[tile, tile],
        out_shape=[jax.ShapeDtypeStruct((M, D), F32), jax.ShapeDtypeStruct((M, D), BF16)],
        compiler_params=_cp("parallel"), name=name)(a, w, x, g.reshape(1, D), b.reshape(1, D))


def _ffn_in_kernel(a_ref, wg_ref, wu_ref, o_ref):
    a = a_ref[...]
    gate = jnp.dot(a, wg_ref[...], preferred_element_type=F32)
    up = jnp.dot(a, wu_ref[...], preferred_element_type=F32)
    o_ref[...] = (_silu(gate) * up).astype(o_ref.dtype)


def _ffn_in(a, w, tm=1024, tn=256):
    M, K = a.shape
    F = w.shape[1] // 2
    nj = F // tn
    return pl.pallas_call(
        _ffn_in_kernel, grid=(M // tm, nj),
        in_specs=[pl.BlockSpec((tm, K), lambda i, j: (i, 0)),
                  pl.BlockSpec((K, tn), lambda i, j: (0, j)),
                  pl.BlockSpec((K, tn), lambda i, j: (0, j + nj))],
        out_specs=pl.BlockSpec((tm, tn), lambda i, j: (i, j)),
        out_shape=jax.ShapeDtypeStruct((M, F), BF16),
        compiler_params=_cp("parallel", "parallel"), name="ffn_in")(a, w, w)


def _rope_rot(v, cos, sin):
    return v * cos + pltpu.roll(v, LANE // 2, 1) * sin


def _mla_proj_kernel(x_ref, wd_ref, qn_ref, kvn_ref, wuq_ref, wukv_ref, cos_ref, sin_ref,
                     q_ref, kv_ref, kr_ref, *, qscale):
    down = jnp.dot(x_ref[...], wd_ref[...], preferred_element_type=F32)
    cq = down[:, :MLA_Q_RANK]
    ckv = down[:, MLA_Q_RANK:MLA_Q_RANK + MLA_KV_RANK]
    kr = down[:, MLA_Q_RANK + MLA_KV_RANK:]
    cos = cos_ref[...]
    sin = sin_ref[...]
    cq = cq * lax.rsqrt(jnp.mean(cq * cq, axis=-1, keepdims=True) + RMS_EPS) * qn_ref[...]
    ckv = ckv * lax.rsqrt(jnp.mean(ckv * ckv, axis=-1, keepdims=True) + RMS_EPS) * kvn_ref[...]
    kr_ref[...] = _rope_rot(kr, cos, sin).astype(BF16)
    q = jnp.dot(cq.astype(BF16), wuq_ref[...], preferred_element_type=F32)
    for h in range(MLA_HEADS):
        lo = h * MLA_QK_PAD
        q_ref[:, lo:lo + MLA_NOPE] = (q[:, lo:lo + MLA_NOPE] * qscale).astype(BF16)
        rot = _rope_rot(q[:, lo + MLA_NOPE:lo + MLA_QK_PAD], cos, sin)
        q_ref[:, lo + MLA_NOPE:lo + MLA_QK_PAD] = (rot * qscale).astype(BF16)
    kv_ref[...] = jnp.dot(ckv.astype(BF16), wukv_ref[...],
                          preferred_element_type=F32).astype(BF16)


def _mla_proj(xb, wd, qn, kvn, wuq, wukv, cos, sin, tm=512):
    T, D = xb.shape
    nd = wd.shape[1]
    nq = wuq.shape[1]
    nkv = wukv.shape[1]
    qscale = (MLA_NOPE + MLA_ROPE) ** -0.5 * 1.4426950408889634
    full = lambda shape: pl.BlockSpec(shape, lambda i: (0, 0))
    rows = lambda n: pl.BlockSpec((tm, n), lambda i: (i, 0))
    return pl.pallas_call(
        functools.partial(_mla_proj_kernel, qscale=qscale), grid=(T // tm,),
        in_specs=[rows(D), full((D, nd)), full((1, MLA_Q_RANK)), full((1, MLA_KV_RANK)),
                  full((MLA_Q_RANK, nq)), full((MLA_KV_RANK, nkv)), rows(LANE), rows(LANE)],
        out_specs=[rows(nq), rows(nkv), rows(LANE)],
        out_shape=[jax.ShapeDtypeStruct((T, nq), BF16), jax.ShapeDtypeStruct((T, nkv), BF16),
                   jax.ShapeDtypeStruct((T, LANE), BF16)],
        compiler_params=_cp("parallel"), name="mla_proj")(
            xb, wd, qn.reshape(1, -1), kvn.reshape(1, -1), wuq, wukv, cos, sin)


def _attn_kernel(q_ref, kn_ref, kr_ref, v_ref, o_ref, *, tk):
    q = q_ref[...]
    tq = q.shape[0]
    nk = kn_ref.shape[0] // tk
    m = jnp.full((tq, 1), -jnp.inf, F32)
    l = jnp.zeros((tq, 1), F32)
    acc = jnp.zeros((tq, MLA_V), F32)
    for c in range(nk):
        rows = slice(c * tk, (c + 1) * tk)
        kc = jnp.concatenate([kn_ref[rows, :], kr_ref[rows, :]], axis=1)
        s = lax.dot_general(q, kc, _NT, preferred_element_type=F32)
        m_new = jnp.maximum(m, jnp.max(s, axis=1, keepdims=True))
        p = jnp.exp2(s - m_new)
        a = jnp.exp2(m - m_new)
        l = a * l + jnp.sum(p, axis=1, keepdims=True)
        acc = a * acc + jnp.dot(p.astype(BF16), v_ref[rows, :], preferred_element_type=F32)
        m = m_new
    o_ref[...] = (acc / l).astype(o_ref.dtype)


def _attention(q, kv, kr, B, S, tq=512, tk=512):
    T = B * S
    nq = S // tq
    return pl.pallas_call(
        functools.partial(_attn_kernel, tk=tk), grid=(B, MLA_HEADS, nq),
        in_specs=[pl.BlockSpec((tq, MLA_QK_PAD), lambda b, h, i: (b * nq + i, h)),
                  pl.BlockSpec((S, MLA_NOPE), lambda b, h, i: (b, 2 * h)),
                  pl.BlockSpec((S, LANE), lambda b, h, i: (b, 0)),
                  pl.BlockSpec((S, MLA_V), lambda b, h, i: (b, 2 * h + 1))],
        out_specs=pl.BlockSpec((tq, MLA_V), lambda b, h, i: (b * nq + i, h)),
        out_shape=jax.ShapeDtypeStruct((T, MLA_HEADS * MLA_V), BF16),
        compiler_params=_cp("parallel", "parallel", "parallel"), name="mla_attn")(q, kv, kr, kv)


def _ret_qk_kernel(a_ref, w_ref, cos_ref, sin_ref, o_ref):
    acc = jnp.dot(a_ref[...], w_ref[...], preferred_element_type=F32)
    sc = jnp.where(pl.program_id(1) == 1, RET_QK_DIM ** -0.5, 1.0).astype(F32)
    cos = cos_ref[...] * sc
    sin = sin_ref[...] * sc
    half = RET_QK_DIM // 2
    for h in range(RET_HEADS):
        lo = h * RET_QK_DIM
        x1 = acc[:, lo:lo + half]
        x2 = acc[:, lo + half:lo + RET_QK_DIM]
        o_ref[:, lo:lo + half] = x1 * cos - x2 * sin
        o_ref[:, lo + half:lo + RET_QK_DIM] = x1 * sin + x2 * cos


def _ret_qk(xb, w, cos, sin, tm=512):
    T, D = xb.shape
    n = RET_HEADS * RET_QK_DIM
    return pl.pallas_call(
        _ret_qk_kernel, grid=(T // tm, 2),
        in_specs=[pl.BlockSpec((tm, D), lambda i, j: (i, 0)),
                  pl.BlockSpec((D, n), lambda i, j: (0, j)),
                  pl.BlockSpec((tm, LANE), lambda i, j: (i, 0)),
                  pl.BlockSpec((tm, LANE), lambda i, j: (i, 0))],
        out_specs=pl.BlockSpec((tm, n), lambda i, j: (i, j)),
        out_shape=jax.ShapeDtypeStruct((T, 2 * n), F32),
        compiler_params=_cp("parallel", "parallel"), name="ret_qk")(xb, w, cos, sin)


def _ret_tables(logit_ref, direction, dmat_s, qd_s, kd_s, cd_s):
    C = RET_CHUNK
    ii = lax.broadcasted_iota(jnp.int32, (C, C), 0).astype(F32)
    jj = lax.broadcasted_iota(jnp.int32, (C, C), 1).astype(F32)
    ri = lax.broadcasted_iota(jnp.int32, (C, 1), 0).astype(F32)
    for h in range(RET_HEADS):
        logit = logit_ref[direction, h]
        lg = _log_sigmoid(jnp.full((C, C), logit, F32))
        lg_col = _log_sigmoid(jnp.full((C, 1), logit, F32))
        lg_row = _log_sigmoid(jnp.full((1, RET_V_DIM), logit, F32))
        if direction == 0:
            diff = ii - jj
            mask = diff >= 0.0
            qd_s[h] = jnp.exp((ri + 1.0) * lg_col)
            kd_s[h] = jnp.exp((C - 1.0 - ri) * lg_col)
        else:
            diff = jj - ii
            mask = diff > 0.0
            qd_s[h] = jnp.exp((C - ri) * lg_col)
            kd_s[h] = jnp.exp(ri * lg_col)
        dmat_s[h] = jnp.where(mask, jnp.exp(jnp.maximum(diff, 0.0) * lg), 0.0)
        cd_s[h] = jnp.exp(float(C) * lg_row)


def _ret_chunk(q_ref, k_ref, v_ref, state, dmat_s, qd_s, kd_s, cd_s, rows, h):
    qh = q_ref[rows, h * RET_QK_DIM:(h + 1) * RET_QK_DIM].astype(BF16)
    kf = k_ref[rows, h * RET_QK_DIM:(h + 1) * RET_QK_DIM]
    vh = v_ref[rows, h * RET_V_DIM:(h + 1) * RET_V_DIM]
    sc = lax.dot_general(qh, kf.astype(BF16), _NT, preferred_element_type=F32) * dmat_s[h]
    intra = jnp.dot(sc.astype(BF16), vh, preferred_element_type=F32)
    st = state[h]
    cross = jnp.dot(qh, st.astype(BF16), preferred_element_type=F32) * qd_s[h]
    kdec = (kf * kd_s[h]).T.astype(BF16)
    state[h] = st * cd_s[h] + jnp.dot(kdec, vh, preferred_element_type=F32)
    return intra + cross


def _ret_fwd_kernel(logit_ref, q_ref, k_ref, v_ref, o_ref, state, dmat_s, qd_s, kd_s, cd_s):
    @pl.when(pl.program_id(1) == 0)
    def _():
        state[...] = jnp.zeros_like(state)
        _ret_tables(logit_ref, 0, dmat_s, qd_s, kd_s, cd_s)

    for c in range(q_ref.shape[0] // RET_CHUNK):
        rows = slice(c * RET_CHUNK, (c + 1) * RET_CHUNK)
        for h in range(RET_HEADS):
            o_ref[rows, h * RET_V_DIM:(h + 1) * RET_V_DIM] = _ret_chunk(
                q_ref, k_ref, v_ref, state, dmat_s, qd_s, kd_s, cd_s, rows, h)


def _ret_bwd_kernel(logit_ref, q_ref, k_ref, v_ref, yf_ref, g_ref, gw_ref, gb_ref, o_ref,
                    state, dmat_s, qd_s, kd_s, cd_s):
    @pl.when(pl.program_id(1) == 0)
    def _():
        state[...] = jnp.zeros_like(state)
        _ret_tables(logit_ref, 1, dmat_s, qd_s, kd_s, cd_s)

    for c in reversed(range(q_ref.shape[0] // RET_CHUNK)):
        rows = slice(c * RET_CHUNK, (c + 1) * RET_CHUNK)
        for h in range(RET_HEADS):
            cols = slice(h * RET_V_DIM, (h + 1) * RET_V_DIM)
            y = _ret_chunk(q_ref, k_ref, v_ref, state, dmat_s, qd_s, kd_s, cd_s, rows, h)
            y = y + yf_ref[rows, cols]
            mu = jnp.mean(y, axis=-1, keepdims=True)
            yc = y - mu
            var = jnp.mean(yc * yc, axis=-1, keepdims=True)
            yn = yc * lax.rsqrt(var + LN_EPS) * gw_ref[:, cols] + gb_ref[:, cols]
            o_ref[rows, cols] = (_silu(g_ref[rows, cols]) * yn).astype(o_ref.dtype)


def _ret_scratch():
    H, C = RET_HEADS, RET_CHUNK
    return [pltpu.VMEM((H, RET_QK_DIM, RET_V_DIM), F32), pltpu.VMEM((H, C, C), F32),
            pltpu.VMEM((H, C, 1), F32), pltpu.VMEM((H, C, 1), F32),
            pltpu.VMEM((H, 1, RET_V_DIM), F32)]


def _ret_scan(logit, qk, v, g, gw, gb, B, S, R=512):
    T = B * S
    nb = S // R
    nqk = RET_HEADS * RET_QK_DIM
    nv = RET_HEADS * RET_V_DIM
    smem = pl.BlockSpec(memory_space=pltpu.SMEM)
    fwd_i = lambda b, j: b * nb + j
    bwd_i = lambda b, j: b * nb + (nb - 1 - j)
    yf = pl.pallas_call(
        _ret_fwd_kernel, grid=(B, nb),
        in_specs=[smem,
                  pl.BlockSpec((R, nqk), lambda b, j: (fwd_i(b, j), 0)),
                  pl.BlockSpec((R, nqk), lambda b, j: (fwd_i(b, j), 1)),
                  pl.BlockSpec((R, nv), lambda b, j: (fwd_i(b, j), 0))],
        out_specs=pl.BlockSpec((R, nv), lambda b, j: (fwd_i(b, j), 0)),
        out_shape=jax.ShapeDtypeStruct((T, nv), F32),
        scratch_shapes=_ret_scratch(),
        compiler_params=_cp("arbitrary", "arbitrary"), name="ret_fwd")(logit, qk, qk, v)
    return pl.pallas_call(
        _ret_bwd_kernel, grid=(B, nb),
        in_specs=[smem,
                  pl.BlockSpec((R, nqk), lambda b, j: (bwd_i(b, j), 0)),
                  pl.BlockSpec((R, nqk), lambda b, j: (bwd_i(b, j), 1)),
                  pl.BlockSpec((R, nv), lambda b, j: (bwd_i(b, j), 0)),
                  pl.BlockSpec((R, nv), lambda b, j: (bwd_i(b, j), 0)),
                  pl.BlockSpec((R, nv), lambda b, j: (bwd_i(b, j), 0)),
                  pl.BlockSpec((1, nv), lambda b, j: (0, 0)),
                  pl.BlockSpec((1, nv), lambda b, j: (0, 0))],
        out_specs=pl.BlockSpec((R, nv), lambda b, j: (bwd_i(b, j), 0)),
        out_shape=jax.ShapeDtypeStruct((T, nv), BF16),
        scratch_shapes=_ret_scratch(),
        compiler_params=_cp("arbitrary", "arbitrary"), name="ret_bwd")(
            logit, qk, qk, v, yf, g, gw.reshape(1, nv), gb.reshape(1, nv))


def _gla_gate_kernel(x_ref, wa1_ref, wa2_ref, ba_ref, o_ref):
    t1 = jnp.dot(x_ref[...], wa1_ref[...], preferred_element_type=F32)
    logits = jnp.dot(t1.astype(BF16), wa2_ref[...], preferred_element_type=F32) + ba_ref[...]
    o_ref[...] = _log_sigmoid(logits) * (1.0 / GLA_TAU)


def _gla_gate(xb, wa1, wa2, ba, tm=1024):
    T, D = xb.shape
    n = wa2.shape[1]
    return pl.pallas_call(
        _gla_gate_kernel, grid=(T // tm,),
        in_specs=[pl.BlockSpec((tm, D), lambda i: (i, 0)),
                  pl.BlockSpec((D, LANE), lambda i: (0, 0)),
                  pl.BlockSpec((LANE, n), lambda i: (0, 0)),
                  pl.BlockSpec((1, n), lambda i: (0, 0))],
        out_specs=pl.BlockSpec((tm, n), lambda i: (i, 0)),
        out_shape=jax.ShapeDtypeStruct((T, n), F32),
        compiler_params=_cp("parallel"), name="gla_gate")(xb, wa1, wa2, ba)


def _gla_chunk(q_ref, k_ref, v_ref, la_ref, state, rows, direction):
    C = GLA_CHUNK
    DK, DV = GLA_KEY_DIM, GLA_V_DIM
    ii = lax.broadcasted_iota(jnp.int32, (C, C), 0)
    jj = lax.broadcasted_iota(jnp.int32, (C, C), 1)
    if direction == 0:
        tri = ii >= jj
        keep = tri
        last = C - 1
    else:
        tri = jj >= ii
        keep = jj > ii
        last = 0
    tri = jnp.where(tri, 1.0, 0.0).astype(BF16)
    la = la_ref[rows, :]
    la_hi = la.astype(BF16)
    la_lo = (la - la_hi.astype(F32)).astype(BF16)
    b = (jnp.dot(tri, la_hi, preferred_element_type=F32)
         + jnp.dot(tri, la_lo, preferred_element_type=F32))
    b_last = b[last:last + 1, :]
    e_pos = jnp.exp(b)
    e_neg = jnp.exp(-b)
    e_rem = jnp.exp(b_last - b)
    e_last = jnp.exp(b_last)
    q_in = (q_ref[rows, :] * (DK ** -0.5) * e_pos).astype(BF16)
    kf = k_ref[rows, :]
    k_in = (kf * e_neg).astype(BF16)
    k_st = (kf * e_rem).astype(BF16)
    outs = []
    for h in range(GLA_HEADS):
        kc = slice(h * DK, (h + 1) * DK)
        vh = v_ref[rows, h * DV:(h + 1) * DV]
        s = lax.dot_general(q_in[:, kc], k_in[:, kc], _NT, preferred_element_type=F32)
        s = jnp.where(keep, s, 0.0)
        intra = jnp.dot(s.astype(BF16), vh, preferred_element_type=F32)
        st = state[h]
        cross = lax.dot_general(q_in[:, kc], st.astype(BF16), _NT, preferred_element_type=F32)
        state[h] = st * e_last[:, kc] + lax.dot_general(vh, k_st[:, kc], _TN,
                                                        preferred_element_type=F32)
        outs.append(intra + cross)
    return outs


def _gla_fwd_kernel(q_ref, k_ref, v_ref, la_ref, o_ref, state):
    @pl.when(pl.program_id(1) == 0)
    def _():
        state[...] = jnp.zeros_like(state)

    for c in range(q_ref.shape[0] // GLA_CHUNK):
        rows = slice(c * GLA_CHUNK, (c + 1) * GLA_CHUNK)
        outs = _gla_chunk(q_ref, k_ref, v_ref, la_ref, state, rows, 0)
        for h in range(GLA_HEADS):
            o_ref[rows, h * GLA_V_DIM:(h + 1) * GLA_V_DIM] = outs[h]


def _gla_bwd_kernel(q_ref, k_ref, v_ref, la_ref, yf_ref, r_ref, ng_ref, o_ref, state):
    @pl.when(pl.program_id(1) == 0)
    def _():
        state[...] = jnp.zeros_like(state)

    for c in reversed(range(q_ref.shape[0] // GLA_CHUNK)):
        rows = slice(c * GLA_CHUNK, (c + 1) * GLA_CHUNK)
        outs = _gla_chunk(q_ref, k_ref, v_ref, la_ref, state, rows, 1)
        for h in range(GLA_HEADS):
            cols = slice(h * GLA_V_DIM, (h + 1) * GLA_V_DIM)
            y = outs[h] + yf_ref[rows, cols]
            yn = y * lax.rsqrt(jnp.mean(y * y, axis=-1, keepdims=True) + RMS_EPS) * ng_ref[...]
            o_ref[rows, cols] = (_silu(r_ref[rows, cols]) * yn).astype(o_ref.dtype)


def _gla_scan(qk, v, la, r, ng, B, S, R=256):
    T = B * S
    nb = S // R
    nk = GLA_HEADS * GLA_KEY_DIM
    nv = GLA_HEADS * GLA_V_DIM
    scratch = [pltpu.VMEM((GLA_HEADS, GLA_V_DIM, GLA_KEY_DIM), F32)]
    fwd_i = lambda b, j: b * nb + j
    bwd_i = lambda b, j: b * nb + (nb - 1 - j)
    yf = pl.pallas_call(
        _gla_fwd_kernel, grid=(B, nb),
        in_specs=[pl.BlockSpec((R, nk), lambda b, j: (fwd_i(b, j), 0)),
                  pl.BlockSpec((R, nk), lambda b, j: (fwd_i(b, j), 1)),
                  pl.BlockSpec((R, nv), lambda b, j: (fwd_i(b, j), 0)),
                  pl.BlockSpec((R, nk), lambda b, j: (fwd_i(b, j), 0))],
        out_specs=pl.BlockSpec((R, nv), lambda b, j: (fwd_i(b, j), 0)),
        out_shape=jax.ShapeDtypeStruct((T, nv), F32),
        scratch_shapes=scratch,
        compiler_params=_cp("arbitrary", "arbitrary"), name="gla_fwd")(qk, qk, v, la)
    return pl.pallas_call(
        _gla_bwd_kernel, grid=(B, nb),
        in_specs=[pl.BlockSpec((R, nk), lambda b, j: (bwd_i(b, j), 0)),
                  pl.BlockSpec((R, nk), lambda b, j: (bwd_i(b, j), 1)),
                  pl.BlockSpec((R, nv), lambda b, j: (bwd_i(b, j), 0)),
                  pl.BlockSpec((R, nk), lambda b, j: (bwd_i(b, j), 1)),
                  pl.BlockSpec((R, nv), lambda b, j: (bwd_i(b, j), 0)),
                  pl.BlockSpec((R, nv), lambda b, j: (bwd_i(b, j), 0)),
                  pl.BlockSpec((1, GLA_V_DIM), lambda b, j: (0, 0))],
        out_specs=pl.BlockSpec((R, nv), lambda b, j: (bwd_i(b, j), 0)),
        out_shape=jax.ShapeDtypeStruct((T, nv), BF16),
        scratch_shapes=scratch,
        compiler_params=_cp("arbitrary", "arbitrary"), name="gla_bwd")(
            qk, qk, v, la, yf, r, ng.reshape(1, GLA_V_DIM))


def _rope_cols(w):
    half = MLA_ROPE // 2
    z = jnp.zeros(w.shape[:-1] + (LANE // 2 - half,), w.dtype)
    return jnp.concatenate([w[..., :half], z, w[..., half:], z], axis=-1)


def _mla_weights(w_down, w_uq):
    base = MLA_Q_RANK + MLA_KV_RANK
    wd = jnp.concatenate([w_down[:, :base], _rope_cols(w_down[:, base:])], axis=1)
    wq = w_uq.reshape(MLA_Q_RANK, MLA_HEADS, MLA_NOPE + MLA_ROPE)
    wq = jnp.concatenate([wq[..., :MLA_NOPE], _rope_cols(wq[..., MLA_NOPE:])], axis=-1)
    return wd.astype(BF16), wq.reshape(MLA_Q_RANK, MLA_HEADS * MLA_QK_PAD).astype(BF16)


def _gla_gate_weights(w_a1, w_a2, b_a):
    r = GLA_GATE_RANK
    nk = GLA_HEADS * GLA_KEY_DIM
    D = w_a1.shape[1]
    wa1 = jnp.zeros((D, LANE), F32).at[:, :r].set(w_a1[0]).at[:, r:2 * r].set(w_a1[1])
    wa2 = jnp.zeros((LANE, 2 * nk), F32).at[:r, :nk].set(w_a2[0]).at[r:2 * r, nk:].set(w_a2[1])
    return wa1.astype(BF16), wa2.astype(BF16), b_a.reshape(1, 2 * nk)


def kernel(x, positions, ln_g, ln_b, ffn_w_in, ffn_w_out, mla_w_down, mla_q_norm, mla_w_uq,
           mla_kv_norm, mla_w_ukv, mla_w_o, ret_w_in, ret_decay_logit, ret_gn_w, ret_gn_b,
           ret_w_o, gla_w_in, gla_w_a1, gla_w_a2, gla_b_a, gla_norm, gla_w_o):
    B, S, D = x.shape
    T = B * S
    xf = x.reshape(T, D)
    xb = xf.astype(BF16)
    pos = positions.reshape(T, 1).astype(F32)

    f_mla = ROPE_BASE ** (-jnp.arange(0, MLA_ROPE, 2, dtype=F32) / MLA_ROPE)
    one = jnp.ones((MLA_ROPE // 2,), F32)
    freq_mla = _rope_cols(jnp.concatenate([f_mla, f_mla])).reshape(1, LANE)
    sign_mla = _rope_cols(jnp.concatenate([-one, one])).reshape(1, LANE)
    cos_mla, sin_mla = _rope_tables(pos, freq_mla, sign_mla)
    freq_ret = (ROPE_BASE ** (-jnp.arange(0, RET_QK_DIM, 2, dtype=F32) / RET_QK_DIM)).reshape(1, LANE)
    cos_ret, sin_ret = _rope_tables(pos, freq_ret, jnp.ones((1, LANE), F32))

    for i in range(DEPTH):
        kind = i % N_MIXERS
        j = i // N_MIXERS
        if kind == 0:
            wd, wq = _mla_weights(mla_w_down[j], mla_w_uq[j])
            q, kv, kr = _mla_proj(xb, wd, mla_q_norm[j], mla_kv_norm[j], wq,
                                  mla_w_ukv[j].astype(BF16), cos_mla, sin_mla)
            a = _attention(q, kv, kr, B, S)
            w_o = mla_w_o[j]
        elif kind == 1:
            w = ret_w_in[j].astype(BF16)
            nqk = RET_HEADS * RET_QK_DIM
            nv = RET_HEADS * RET_V_DIM
            qk = _ret_qk(xb, w, cos_ret, sin_ret)
            v = _mm(xb, w, col_off=2 * nqk, n=nv, out_dtype=BF16, name="ret_v")
            g = _mm(xb, w, col_off=2 * nqk + nv, n=nv, out_dtype=F32, name="ret_g")
            a = _ret_scan(ret_decay_logit[j], qk, v, g, ret_gn_w[j], ret_gn_b[j], B, S)
            w_o = ret_w_o[j]
        else:
            w = gla_w_in[j].astype(BF16)
            nk = GLA_HEADS * GLA_KEY_DIM
            nv = GLA_HEADS * GLA_V_DIM
            qk = _mm(xb, w, col_off=0, n=2 * nk, out_dtype=F32, name="gla_qk")
            v = _mm(xb, w, col_off=2 * nk, n=nv, out_dtype=BF16, name="gla_v")
            r = _mm(xb, w, col_off=2 * nk + nv, n=nv, out_dtype=F32, name="gla_r")
            la = _gla_gate(xb, *_gla_gate_weights(gla_w_a1[j], gla_w_a2[j], gla_b_a[j]))
            a = _gla_scan(qk, v, la, r, gla_norm[j], B, S)
            w_o = gla_w_o[j]
        xf, xb = _mm_ln(a, w_o.astype(BF16), xf, ln_g[i, 0], ln_b[i, 0], name="mixer_out_ln")
        hid = _ffn_in(xb, ffn_w_in[i].astype(BF16))
        xf, xb = _mm_ln(hid, ffn_w_out[i].astype(BF16), xf, ln_g[i, 1], ln_b[i, 1],
                        name="ffn_out_ln")
    return xf.reshape(B, S, D)
```
